```python
import jax, jax.numpy as jnp
from jax import lax
import numpy as np

D_MODEL = 1024
BATCH = 32
SEQ = 256
DEPTH = 1
DEC_BATCH = 2
DEC_SEQ = 2048
PAST_LEN = 512

GRID_W = 64
F_GROUPS = 4
F_GROUP_W = D_MODEL // 4
F_WIDTH = F_GROUPS * F_GROUP_W
D_INNER = 2 * D_MODEL
HEAD_DIM = 64
N_HEADS = D_INNER // HEAD_DIM
N_BC_GROUPS = 4
HEADS_PER_GROUP = N_HEADS // N_BC_GROUPS
D_STATE = 128
CONV_W = 5
CHUNK = 128
D_FF = 4 * D_MODEL
N_BRANCH = 2
N_MOD = 6
D_XBC = D_INNER + 2 * N_BC_GROUPS * D_STATE
D_IN_PROJ = F_WIDTH + D_INNER + D_XBC + 2 * N_HEADS + N_BRANCH * D_MODEL
SPLITS = (F_WIDTH, F_WIDTH + D_INNER, F_WIDTH + D_INNER + D_XBC,
          F_WIDTH + D_INNER + D_XBC + 2 * N_HEADS)
EPS = 1e-6

kernel_name = "hybrid_fnet_bissd_dit_step"


def rmsnorm(x, g):
    xf = x.astype(jnp.float32)
    y = xf * lax.rsqrt(jnp.mean(xf * xf, axis=-1, keepdims=True) + EPS)
    return (y * g.astype(jnp.float32)).astype(x.dtype)


def dwconv_centred(u, w, b):
    y = lax.conv_general_dilated(
        u, w[:, None, :].astype(u.dtype), window_strides=(1,),
        padding=[(CONV_W // 2, CONV_W // 2)],
        dimension_numbers=('NWC', 'WIO', 'NWC'), feature_group_count=u.shape[-1])
    return y + b.astype(u.dtype)


def fourier_mix(u, grid):
    b, l, _ = u.shape
    uf = u.astype(jnp.float32).reshape(b, l, F_GROUPS, F_GROUP_W)
    if grid:
        rows = l // GRID_W
        uf = uf.reshape(b, rows, GRID_W, F_GROUPS, F_GROUP_W)
        y = jnp.fft.fftn(uf, axes=(1, 2, 4), norm="ortho").real
    else:
        y = jnp.fft.fftn(uf, axes=(1, 3), norm="ortho").real
    return y.reshape(b, l, F_WIDTH).astype(u.dtype)


def ssd_scan(xh, dt, A, Bm, Cm, h0):
    b, l = xh.shape[:2]
    nc = l // CHUNK
    G, R = N_BC_GROUPS, HEADS_PER_GROUP
    x = xh.reshape(b, nc, CHUNK, G, R, HEAD_DIM)
    dtc = dt.reshape(b, nc, CHUNK, G, R)
    acum = jnp.cumsum(dtc * A.reshape(G, R), axis=2)
    xdt = x * dtc[..., None]
    Bc = Bm.reshape(b, nc, CHUNK, G, D_STATE)
    Cc = Cm.reshape(b, nc, CHUNK, G, D_STATE)
    seg = acum[:, :, :, None] - acum[:, :, None, :]
    lower = jnp.tril(jnp.ones((CHUNK, CHUNK), dtype=bool))[:, :, None, None]
    Lmat = jnp.exp(jnp.where(lower, seg, -jnp.inf))
    cb = jnp.einsum('bcign,bcjgn->bcijg', Cc, Bc)
    y_diag = jnp.einsum('bcijg,bcijgr,bcjgrp->bcigrp', cb, Lmat, xdt)
    decay = jnp.exp(acum[:, :, -1:] - acum)
    states = jnp.einsum('bcjgn,bcjgr,bcjgrp->bcgrpn', Bc, decay, xdt)
    chunk_decay = jnp.exp(acum[:, :, -1])

    def step(h, inp):
        s, dcy = inp
        return h * dcy[..., None, None] + s, h

    h0g = h0.reshape(b, G, R, HEAD_DIM, D_STATE)
    h_fin, h_enter = lax.scan(step, h0g, (jnp.moveaxis(states, 1, 0), jnp.moveaxis(chunk_decay, 1, 0)))
    h_enter = jnp.moveaxis(h_enter, 0, 1)
    y_off = jnp.einsum('bcign,bcgrpn,bcigr->bcigrp', Cc, h_enter, jnp.exp(acum))
    y = (y_diag + y_off).reshape(b, l, N_HEADS, HEAD_DIM)
    return y, h_fin.reshape(b, N_HEADS, HEAD_DIM, D_STATE)


def ssd_mixer(z, xbc, dt_raw, h_f0, h_b0, conv_w, conv_b, dt_bias, A_log, D_skip, norm_g):
    f32 = jnp.float32
    b, l, _ = xbc.shape
    xbc = jax.nn.silu(dwconv_centred(xbc, conv_w, conv_b))
    xs, Bm, Cm = jnp.split(xbc, [D_INNER, D_INNER + N_BC_GROUPS * D_STATE], axis=-1)
    xh = xs.astype(f32).reshape(b, l, N_HEADS, HEAD_DIM)
    Bm = Bm.astype(f32).reshape(b, l, N_BC_GROUPS, D_STATE)
    Cm = Cm.astype(f32).reshape(b, l, N_BC_GROUPS, D_STATE)
    dt = jax.nn.softplus(dt_raw.astype(f32).reshape(b, l, 2, N_HEADS) + dt_bias.astype(f32))
    A = -jnp.exp(A_log.astype(f32))
    y_f, h_f = ssd_scan(xh, dt[:, :, 0], A[0], Bm, Cm, h_f0.astype(f32))
    flip = lambda t: jnp.flip(t, axis=1)
    y_b, h_b = ssd_scan(flip(xh), flip(dt[:, :, 1]), A[1], flip(Bm), flip(Cm), h_b0.astype(f32))
    y = y_f + flip(y_b) + xh * D_skip.astype(f32)[:, None]
    y = y.reshape(b, l, D_INNER) * jax.nn.silu(z.astype(f32))
    return rmsnorm(y, norm_g).astype(z.dtype), h_f, h_b


def layer(x, mod, h_f0, h_b0, grid, norm1_g, w_in, w_fourier, conv_w, conv_b, dt_bias, A_log,
          D_skip, ssd_norm_g, w_ssd_out, w_out, norm2_g, w_ff1, w_ff2):
    shift1, scale1, gate1, shift2, scale2, gate2 = [mod[:, k][:, None] for k in range(N_MOD)]
    h = rmsnorm(x, norm1_g) * (1 + scale1) + shift1
    proj = h @ w_in
    u_f, z, xbc, dt_raw, gates = jnp.split(proj, list(SPLITS), axis=-1)
    y_f = fourier_mix(u_f, grid) @ w_fourier
    y_s, h_f, h_b = ssd_mixer(z, xbc, dt_raw, h_f0, h_b0, conv_w, conv_b, dt_bias, A_log,
                              D_skip, ssd_norm_g)
    y_s = y_s @ w_ssd_out
    g_f, g_s = jnp.split(jax.nn.sigmoid(gates), N_BRANCH, axis=-1)
    x = x + gate1 * ((g_f * y_f + g_s * y_s) @ w_out)
    h2 = rmsnorm(x, norm2_g) * (1 + scale2) + shift2
    x = x + gate2 * (jnp.square(jax.nn.relu(h2 @ w_ff1)) @ w_ff2)
    return x, h_f, h_b


def setup_inputs(seed: int = 0) -> dict:
    key = jax.random.key(seed)
    ks = jax.random.split(key, 32)
    f32 = jnp.float32
    nrm = lambda k, shape, s: jax.random.normal(k, shape, f32) * s
    dt0 = jnp.exp(jax.random.uniform(ks[10], (DEPTH, 2, N_HEADS), f32, np.log(1e-3), np.log(1e-1)))
    return {
        "x_prompt": nrm(ks[0], (BATCH, SEQ, D_MODEL), 1.0),
        "x_sample": nrm(ks[1], (DEC_BATCH, DEC_SEQ, D_MODEL), 1.0),
        "state_ssm_fwd": nrm(ks[2], (DEC_BATCH, DEPTH, N_HEADS, HEAD_DIM, D_STATE), 0.1),
        "state_ssm_bwd": nrm(ks[3], (DEC_BATCH, DEPTH, N_HEADS, HEAD_DIM, D_STATE), 0.1),
        "c": nrm(ks[4], (DEC_BATCH, D_MODEL), 1.0),
        "c_ctx": nrm(ks[5], (D_MODEL,), 1.0),
        "w_mod": nrm(ks[6], (DEPTH, D_MODEL, N_MOD * D_MODEL), D_MODEL ** -0.5),
        "b_mod": nrm(ks[7], (DEPTH, N_MOD * D_MODEL), 0.02),
        "norm1_g": 1.0 + nrm(ks[8], (DEPTH, D_MODEL), 0.02),
        "w_in": nrm(ks[9], (DEPTH, D_MODEL, D_IN_PROJ), D_MODEL ** -0.5),
        "w_fourier": nrm(ks[11], (DEPTH, F_WIDTH, D_MODEL), F_WIDTH ** -0.5),
        "conv_w": nrm(ks[12], (DEPTH, CONV_W, D_XBC), CONV_W ** -0.5),
        "conv_b": nrm(ks[13], (DEPTH, D_XBC), 0.02),
        "dt_bias": dt0 + jnp.log(-jnp.expm1(-dt0)),
        "A_log": jnp.log(jax.random.uniform(ks[14], (DEPTH, 2, N_HEADS), f32, 1.0, 16.0)),
        "D_skip": 1.0 + nrm(ks[15], (DEPTH, N_HEADS), 0.1),
        "ssd_norm_g": 1.0 + nrm(ks[16], (DEPTH, D_INNER), 0.02),
        "w_ssd_out": nrm(ks[17], (DEPTH, D_INNER, D_MODEL), D_INNER ** -0.5),
        "w_out": nrm(ks[18], (DEPTH, D_MODEL, D_MODEL), D_MODEL ** -0.5),
        "norm2_g": 1.0 + nrm(ks[19], (DEPTH, D_MODEL), 0.02),
        "w_ff1": nrm(ks[20], (DEPTH, D_MODEL, D_FF), D_MODEL ** -0.5),
        "w_ff2": nrm(ks[21], (DEPTH, D_FF, D_MODEL), D_FF ** -0.5),
        "final_norm_g": 1.0 + nrm(ks[22], (D_MODEL,), 0.02),
    }


def reference(x_prompt, x_sample, state_ssm_fwd, state_ssm_bwd, c, c_ctx, w_mod, b_mod, norm1_g,
              w_in, w_fourier, conv_w, conv_b, dt_bias, A_log, D_skip, ssd_norm_g, w_ssd_out,
              w_out, norm2_g, w_ff1, w_ff2, final_norm_g):
    bp = x_prompt.shape[0]
    xp, xs = x_prompt, x_sample
    zeros = jnp.zeros((bp, N_HEADS, HEAD_DIM, D_STATE), jnp.float32)
    new_f, new_b = [], []
    for i in range(DEPTH):
        mod_ctx = (jax.nn.silu(c_ctx)[None] @ w_mod[i] + b_mod[i]).reshape(1, N_MOD, D_MODEL)
        mod_lat = (jax.nn.silu(c) @ w_mod[i] + b_mod[i]).reshape(c.shape[0], N_MOD, D_MODEL)
        lw = (norm1_g[i], w_in[i], w_fourier[i], conv_w[i], conv_b[i], dt_bias[i], A_log[i],
              D_skip[i], ssd_norm_g[i], w_ssd_out[i], w_out[i], norm2_g[i], w_ff1[i], w_ff2[i])
        xp, hf, hb = layer(xp, mod_ctx, zeros, zeros, False, *lw)
        new_f.append(hf.astype(x_prompt.dtype))
        new_b.append(hb.astype(x_prompt.dtype))
        xs, _, _ = layer(xs, mod_lat, state_ssm_fwd[:, i], state_ssm_bwd[:, i], True, *lw)
    y_prompt = rmsnorm(xp, final_norm_g)
    y_sample = rmsnorm(xs, final_norm_g)
    new_state_fwd = jnp.stack(new_f, axis=1)
    new_state_bwd = jnp.stack(new_b, axis=1)
    return (y_prompt, y_sample, new_state_fwd, new_state_bwd)
```

```python
import functools

import numpy as np
import jax
import jax.numpy as jnp
from jax import lax
from jax.experimental import pallas as pl
from jax.experimental.pallas import tpu as pltpu

D_MODEL = 1024
GRID_W = 64
F_GROUPS = 4
F_GROUP_W = D_MODEL // F_GROUPS
F_WIDTH = F_GROUPS * F_GROUP_W
D_INNER = 2 * D_MODEL
HEAD_DIM = 64
N_HEADS = D_INNER // HEAD_DIM
N_BC_GROUPS = 4
HEADS_PER_GROUP = N_HEADS // N_BC_GROUPS
D_STATE = 128
CONV_W = 5
CHUNK = 128
D_FF = 4 * D_MODEL
N_MOD = 6
D_BC = N_BC_GROUPS * D_STATE
D_XBC = D_INNER + 2 * D_BC
EPS = 1e-6

LANES = 128
SUBLANES = 8
TOKEN_TILE = 256
HEADS_PER_DOT = 4
VMEM_LIMIT = 60000 * 1024

F32 = jnp.float32
BF16 = jnp.bfloat16


def _const_spec(shape):
    return pl.BlockSpec(shape, lambda *_: (0,) * len(shape), pipeline_mode=pl.Buffered(1))


def _params(n_axes):
    return pltpu.CompilerParams(dimension_semantics=("arbitrary",) * n_axes,
                                vmem_limit_bytes=VMEM_LIMIT)


def _dot(a, b):
    return jnp.dot(a, b, preferred_element_type=F32)


def _rms(x):
    return x * lax.rsqrt(jnp.mean(x * x, axis=-1, keepdims=True) + EPS)


def _split3(v):
    hi = v.astype(BF16)
    r1 = v - hi.astype(F32)
    mid = r1.astype(BF16)
    lo = (r1 - mid.astype(F32)).astype(BF16)
    return hi, mid, lo


def _dft(n):
    k = np.arange(n)
    ang = 2.0 * np.pi * np.outer(k, k) / n
    return np.cos(ang) / np.sqrt(n), np.sin(ang) / np.sqrt(n)


def _mod_kernel(c_ref, w_ref, b_ref, o_ref):
    c = c_ref[...]
    s = (c * jax.nn.sigmoid(c)).astype(BF16)
    o_ref[...] = _dot(s, w_ref[...].astype(BF16)) + b_ref[...]


def _mod(cvec, w_mod, b_mod):
    n = w_mod.shape[1]
    tn = 1536
    return pl.pallas_call(
        _mod_kernel,
        grid=(n // tn,),
        in_specs=[pl.BlockSpec((SUBLANES, D_MODEL), lambda j: (0, 0)),
                  pl.BlockSpec((D_MODEL, tn), lambda j: (0, j)),
                  pl.BlockSpec((1, tn), lambda j: (0, j))],
        out_specs=pl.BlockSpec((SUBLANES, tn), lambda j: (0, j)),
        out_shape=jax.ShapeDtypeStruct((SUBLANES, n), F32),
        compiler_params=_params(1),
        name="mod",
    )(cvec, w_mod, b_mod)


def _in_proj_kernel(n_f, x_ref, mod_ref, g1_ref, wuf_ref, wz_ref, wxbc_ref, wdt_ref, wg_ref,
                    cs_ref, gpos_ref, z_ref, xbc_ref, dt_ref, gate_ref, *f_refs):
    x = x_ref[...]
    m = mod_ref[0]
    shift1, scale1 = m[0:1], m[1:2]
    h = ((_rms(x) * g1_ref[...]) * (1.0 + scale1) + shift1).astype(BF16)
    z_ref[...] = _dot(h, wz_ref[...])
    xbc_ref[...] = _dot(h, wxbc_ref[...])
    dt_ref[...] = _dot(h, wdt_ref[...])
    gate_ref[...] = _dot(h, wg_ref[...])
    uf = _dot(h, wuf_ref[...]).astype(BF16)
    ucs = [_dot(uf[:, k * F_GROUP_W:(k + 1) * F_GROUP_W], cs_ref[...]) for k in range(F_GROUPS)]
    uc = jnp.concatenate([t[:, :F_GROUP_W] for t in ucs], axis=1)
    us = jnp.concatenate([t[:, F_GROUP_W:] for t in ucs], axis=1)
    stacked = jnp.concatenate([uc, us], axis=0).astype(BF16)
    f = _dot(gpos_ref[...], stacked)
    for k in range(n_f):
        f_refs[k][...] = f[k * TOKEN_TILE:(k + 1) * TOKEN_TILE]


def _in_proj(x2d, mod3, mod_row, g1, wuf, wz, wxbc, wdt, wg, cs, gpos):
    t = x2d.shape[0]
    n_f = gpos.shape[0] // TOKEN_TILE
    row = lambda i: (i, 0)
    outs = [(D_INNER, F32), (D_XBC, F32), (LANES, F32), (D_INNER, F32)] + [(F_WIDTH, F32)] * n_f
    return pl.pallas_call(
        functools.partial(_in_proj_kernel, n_f),
        grid=(t // TOKEN_TILE,),
        in_specs=[pl.BlockSpec((TOKEN_TILE, D_MODEL), row),
                  pl.BlockSpec((1, N_MOD, D_MODEL), lambda i: (mod_row(i), 0, 0)),
                  _const_spec(g1.shape), _const_spec(wuf.shape), _const_spec(wz.shape),
                  _const_spec(wxbc.shape), _const_spec(wdt.shape), _const_spec(wg.shape),
                  _const_spec(cs.shape), _const_spec(gpos.shape)],
        out_specs=[pl.BlockSpec((TOKEN_TILE, w), row) for w, _ in outs],
        out_shape=[jax.ShapeDtypeStruct((t, w), d) for w, d in outs],
        compiler_params=_params(1),
        name="in_proj",
    )(x2d, mod3, g1, wuf, wz, wxbc, wdt, wg, cs, gpos)


def _row_dft_kernel(c_ref, s_ref, wr_ref, wi_ref, o_ref):
    o_ref[0] = (_dot(c_ref[...], wr_ref[0].astype(BF16)) + _dot(s_ref[...], wi_ref[0].astype(BF16)))


def _row_dft(c32, s32, wr, wi):
    b, r, n = wr.shape
    tn = 8192
    blk = pl.BlockSpec((1, r, tn), lambda i, j: (i, 0, j))
    return pl.pallas_call(
        _row_dft_kernel,
        grid=(b, n // tn),
        in_specs=[_const_spec(c32.shape), _const_spec(s32.shape), blk, blk],
        out_specs=blk,
        out_shape=jax.ShapeDtypeStruct((b, r, n), F32),
        compiler_params=_params(2),
        name="row_dft",
    )(c32, s32, wr, wi)


def _ssd_kernel(rev, nc, has_h0, want_state, *refs):
    refs = list(refs)
    main_ref, prev_ref, next_ref, dt_ref, cw_ref, cb_ref, dtb_ref, alog_ref, dskip_ref = refs[:9]
    refs = refs[9:]
    h0_ref = refs.pop(0) if has_h0 else None
    y_ref = refs.pop(0)
    hfin_ref = refs.pop(0) if want_state else None
    ht_ref, ext_ref = refs

    c = pl.program_id(1)
    cidx = (nc - 1 - c) if rev else c
    last = 0 if rev else CHUNK - 1
    lo = N_HEADS if rev else 0

    @pl.when(c == 0)
    def _():
        ht_ref[...] = h0_ref[0] if has_h0 else jnp.zeros_like(ht_ref)

    ext_ref[0:SUBLANES] = jnp.where(cidx > 0, prev_ref[...], 0.0)
    ext_ref[SUBLANES:SUBLANES + CHUNK] = main_ref[...]
    ext_ref[SUBLANES + CHUNK:2 * SUBLANES + CHUNK] = jnp.where(cidx < nc - 1, next_ref[...], 0.0)
    acc = cb_ref[...] + cw_ref[0:1] * ext_ref[pl.ds(SUBLANES - CONV_W // 2, CHUNK), :]
    for k in range(1, CONV_W):
        acc = acc + cw_ref[k:k + 1] * ext_ref[pl.ds(SUBLANES - CONV_W // 2 + k, CHUNK), :]
    xc = acc * jax.nn.sigmoid(acc)
    xs = xc[:, :D_INNER]
    xs_b = xs.astype(BF16)
    bm = xc[:, D_INNER:D_INNER + D_BC]
    cm_b = xc[:, D_INNER + D_BC:].astype(BF16)

    v = dt_ref[...] + dtb_ref[...]
    dt = jnp.maximum(v, 0.0) + jnp.log1p(jnp.exp(-jnp.abs(v)))
    a = dt * (-jnp.exp(alog_ref[...]))
    ii = lax.broadcasted_iota(jnp.int32, (CHUNK, CHUNK), 0)
    jj = lax.broadcasted_iota(jnp.int32, (CHUNK, CHUNK), 1)
    mask = (jj >= ii) if rev else (jj <= ii)
    tri = mask.astype(BF16)
    acum = sum(_dot(tri, p) for p in _split3(a))
    acum_t = acum.T
    dt_t = dt.T
    w_t = dt_t * jnp.exp(acum_t[:, last:last + 1] - acum_t)
    lane_lo = jj < HEAD_DIM

    for g in range(N_BC_GROUPS):
        gs = slice(g * D_STATE, (g + 1) * D_STATE)
        b_g = bm[:, gs]
        c_g = cm_b[:, gs]
        cb = lax.dot_general(c_g, b_g.astype(BF16), (((1,), (1,)), ((), ())),
                             preferred_element_type=F32)
        bt_g = b_g.T
        hcols = slice(g * HEADS_PER_GROUP * HEAD_DIM, (g + 1) * HEADS_PER_GROUP * HEAD_DIM)
        y_off = _dot(c_g, ht_ref[:, hcols].astype(BF16))
        for q in range(HEADS_PER_GROUP // HEADS_PER_DOT):
            h0 = g * HEADS_PER_GROUP + q * HEADS_PER_DOT
            ms, bts, ebs = [], [], []
            for r in range(HEADS_PER_DOT):
                hl = lo + h0 + r
                colb = jnp.broadcast_to(acum[:, hl:hl + 1], (CHUNK, CHUNK))
                seg = colb - acum_t[hl:hl + 1, :]
                l_mat = jnp.exp(jnp.where(mask, seg, -jnp.inf))
                ms.append((cb * l_mat * dt_t[hl:hl + 1, :]).astype(BF16))
                bts.append((bt_g * w_t[hl:hl + 1, :]).astype(BF16))
                ebs.append(jnp.exp(colb))
            cols = slice(h0 * HEAD_DIM, (h0 + HEADS_PER_DOT) * HEAD_DIM)
            xq = xs_b[:, cols]
            wq = HEADS_PER_DOT * HEAD_DIM
            ri = lax.broadcasted_iota(jnp.int32, (HEADS_PER_DOT * CHUNK, wq), 0) // CHUNK
            ci = lax.broadcasted_iota(jnp.int32, (HEADS_PER_DOT * CHUNK, wq), 1) // HEAD_DIM
            xbd = jnp.where(ri == ci, jnp.concatenate([xq] * HEADS_PER_DOT, axis=0),
                            jnp.zeros((), BF16))
            y_diag = _dot(jnp.concatenate(ms, axis=1), xbd)
            s_new = _dot(jnp.concatenate(bts, axis=1), xbd)
            e_exp = jnp.concatenate([jnp.where(lane_lo, ebs[0], ebs[1]),
                                     jnp.where(lane_lo, ebs[2], ebs[3])], axis=1)
            y = y_diag + y_off[:, q * wq:(q + 1) * wq] * e_exp
            if not rev:
                y = y + xs[:, cols] * dskip_ref[:, cols]
            y_ref[:, cols] = y
            ht_ref[:, cols] = ht_ref[:, cols] * e_exp[last:last + 1, :] + s_new

    if want_state:
        @pl.when(c == nc - 1)
        def _():
            hfin_ref[0] = ht_ref[...].T


def _ssd(xbc, dtp, conv_w, conv_b, dtb, alog, dskip, h0, nseq, nc, rev, want_state):
    t = xbc.shape[0]
    t8 = t // SUBLANES
    per = CHUNK // SUBLANES
    blk = (lambda s, c: s * nc + (nc - 1 - c)) if rev else (lambda s, c: s * nc + c)
    in_specs = [pl.BlockSpec((CHUNK, D_XBC), lambda s, c: (blk(s, c), 0)),
                pl.BlockSpec((SUBLANES, D_XBC), lambda s, c: (jnp.maximum(blk(s, c) * per - 1, 0), 0)),
                pl.BlockSpec((SUBLANES, D_XBC), lambda s, c: (jnp.minimum((blk(s, c) + 1) * per, t8 - 1), 0)),
                pl.BlockSpec((CHUNK, LANES), lambda s, c: (blk(s, c), 0)),
                _const_spec(conv_w.shape), _const_spec(conv_b.shape), _const_spec(dtb.shape),
                _const_spec(alog.shape), _const_spec(dskip.shape)]
    args = [xbc, xbc, xbc, dtp, conv_w, conv_b, dtb, alog, dskip]
    if h0 is not None:
        in_specs.append(pl.BlockSpec((1, D_STATE, D_INNER), lambda s, c: (s, 0, 0)))
        args.append(h0)
    out_specs = [pl.BlockSpec((CHUNK, D_INNER), lambda s, c: (blk(s, c), 0))]
    out_shape = [jax.ShapeDtypeStruct((t, D_INNER), F32)]
    if want_state:
        out_specs.append(pl.BlockSpec((1, D_INNER, D_STATE), lambda s, c: (s, 0, 0)))
        out_shape.append(jax.ShapeDtypeStruct((nseq, D_INNER, D_STATE), F32))
    return pl.pallas_call(
        functools.partial(_ssd_kernel, rev, nc, h0 is not None, want_state),
        grid=(nseq, nc),
        in_specs=in_specs,
        out_specs=out_specs,
        out_shape=out_shape,
        scratch_shapes=[pltpu.VMEM((D_STATE, D_INNER), F32),
                        pltpu.VMEM((CHUNK + 2 * SUBLANES, D_XBC), F32)],
        compiler_params=_params(2),
        name="ssd_bwd" if rev else "ssd_fwd",
    )(*args)


def _mix_out_kernel(yf_ref, yb_ref, z_ref, f_ref, gate_ref, x_ref, mod_ref, ng_ref,
                    wso_ref, wfo_ref, wo_ref, o_ref):
    z = z_ref[...]
    y = (yf_ref[...] + yb_ref[...]) * (z * jax.nn.sigmoid(z))
    yn = (_rms(y) * ng_ref[...]).astype(BF16)
    y_s = _dot(yn, wso_ref[...])
    y_f = _dot(f_ref[...].astype(BF16), wfo_ref[...])
    gates = jax.nn.sigmoid(gate_ref[...])
    merged = (gates[:, :D_MODEL] * y_f + gates[:, D_MODEL:] * y_s).astype(BF16)
    gate1 = mod_ref[0][2:3]
    o_ref[...] = x_ref[...] + gate1 * _dot(merged, wo_ref[...])


def _mix_out(yf, yb, z, f, gates, x2d, mod3, mod_row, ng, wso, wfo, wo):
    t = x2d.shape[0]
    row = lambda i: (i, 0)
    return pl.pallas_call(
        _mix_out_kernel,
        grid=(t // TOKEN_TILE,),
        in_specs=[pl.BlockSpec((TOKEN_TILE, D_INNER), row), pl.BlockSpec((TOKEN_TILE, D_INNER), row),
                  pl.BlockSpec((TOKEN_TILE, D_INNER), row), pl.BlockSpec((TOKEN_TILE, F_WIDTH), row),
                  pl.BlockSpec((TOKEN_TILE, D_INNER), row), pl.BlockSpec((TOKEN_TILE, D_MODEL), row),
                  pl.BlockSpec((1, N_MOD, D_MODEL), lambda i: (mod_row(i), 0, 0)),
                  _const_spec(ng.shape), _const_spec(wso.shape), _const_spec(wfo.shape),
                  _const_spec(wo.shape)],
        out_specs=pl.BlockSpec((TOKEN_TILE, D_MODEL), row),
        out_shape=jax.ShapeDtypeStruct((t, D_MODEL), F32),
        compiler_params=_params(1),
        name="mix_out",
    )(yf, yb, z, f, gates, x2d, mod3, ng, wso, wfo, wo)


def _mlp_kernel(x_ref, mod_ref, g2_ref, w1_ref, w2_ref, gfin_ref, o_ref):
    x = x_ref[...]
    m = mod_ref[0]
    shift2, scale2, gate2 = m[3:4], m[4:5], m[5:6]
    h = ((_rms(x) * g2_ref[...]) * (1.0 + scale2) + shift2).astype(BF16)
    a = jnp.maximum(_dot(h, w1_ref[...]), 0.0)
    hid = (a * a).astype(BF16)
    x2 = x + gate2 * _dot(hid, w2_ref[...])
    o_ref[...] = _rms(x2) * gfin_ref[...]


def _mlp(x2d, mod3, mod_row, g2, w1, w2, gfin):
    t = x2d.shape[0]
    row = lambda i: (i, 0)
    return pl.pallas_call(
        _mlp_kernel,
        grid=(t // TOKEN_TILE,),
        in_specs=[pl.BlockSpec((TOKEN_TILE, D_MODEL), row),
                  pl.BlockSpec((1, N_MOD, D_MODEL), lambda i: (mod_row(i), 0, 0)),
                  _const_spec(g2.shape), _const_spec(w1.shape), _const_spec(w2.shape),
                  _const_spec(gfin.shape)],
        out_specs=pl.BlockSpec((TOKEN_TILE, D_MODEL), row),
        out_shape=jax.ShapeDtypeStruct((t, D_MODEL), F32),
        compiler_params=_params(1),
        name="mlp",
    )(x2d, mod3, g2, w1, w2, gfin)


def _layer(x3d, mod3, mod_row, h0_f, h0_b, grid, want_state, w):
    nseq, seq, _ = x3d.shape
    x2d = x3d.reshape(nseq * seq, D_MODEL)
    nc = seq // CHUNK
    gpos = w["gpos_grid"] if grid else w["gpos_seq"]
    z, xbc, dtp, gates, *fs = _in_proj(x2d, mod3, mod_row, w["g1"], w["wuf"], w["wz"], w["wxbc"],
                                       w["wdt"], w["wg"], w["cs"], gpos)
    if grid:
        rows = seq // GRID_W
        wr = fs[0].reshape(nseq, rows, GRID_W * F_WIDTH)
        wi = fs[1].reshape(nseq, rows, GRID_W * F_WIDTH)
        f = _row_dft(w["c_rows"], w["s_rows"], wr, wi).reshape(nseq * seq, F_WIDTH)
    else:
        f = fs[0]
    ssd_args = (xbc, dtp, w["conv_w"], w["conv_b"], w["dtb"], w["alog"], w["dskip"])
    out_f = _ssd(*ssd_args, h0_f, nseq, nc, False, want_state)
    out_b = _ssd(*ssd_args, h0_b, nseq, nc, True, want_state)
    x1 = _mix_out(out_f[0], out_b[0], z, f, gates, x2d, mod3, mod_row, w["ng"], w["wso"], w["wfo"],
                  w["wo"])
    y = _mlp(x1, mod3, mod_row, w["g2"], w["w1"], w["w2"], w["gfin"]).reshape(x3d.shape)
    if want_state:
        shape = (nseq, 1, N_HEADS, HEAD_DIM, D_STATE)
        return y, out_f[1].reshape(shape), out_b[1].reshape(shape)
    return y, None, None


def _prep_state(s):
    b = s.shape[0]
    return jnp.transpose(s, (0, 3, 1, 2)).reshape(b, D_STATE, D_INNER)


def kernel(x_prompt, x_sample, state_ssm_fwd, state_ssm_bwd, c, c_ctx, w_mod, b_mod, norm1_g, w_in,
           w_fourier, conv_w, conv_b, dt_bias, A_log, D_skip, ssd_norm_g, w_ssd_out, w_out, norm2_g,
           w_ff1, w_ff2, final_norm_g):
    depth = w_mod.shape[0]
    assert depth == 1
    i = 0
    nb = c.shape[0]
    cvec = jnp.concatenate([c_ctx[None], c, jnp.zeros((SUBLANES - 1 - nb, D_MODEL), F32)], axis=0)
    mod = _mod(cvec, w_mod[i], b_mod[i][None])
    mod3 = mod[:1 + nb].reshape(1 + nb, N_MOD, D_MODEL)

    c_ch, s_ch = _dft(F_GROUP_W)
    c_seq, s_seq = _dft(x_prompt.shape[1])
    c_col, s_col = _dft(GRID_W)
    rows = x_sample.shape[1] // GRID_W
    c_row, s_row = _dft(rows)
    eye = np.eye(TOKEN_TILE // GRID_W)
    bc, bs = np.kron(eye, c_col), np.kron(eye, s_col)
    wi = w_in[i]
    s0, s1, s2, s3 = F_WIDTH, F_WIDTH + D_INNER, F_WIDTH + D_INNER + D_XBC, F_WIDTH + D_INNER + D_XBC + 2 * N_HEADS
    pad = LANES - 2 * N_HEADS
    row = lambda v: v.reshape(1, -1).astype(F32)
    w = {
        "g1": row(norm1_g[i]),
        "wuf": wi[:, :s0].astype(BF16), "wz": wi[:, s0:s1].astype(BF16),
        "wxbc": wi[:, s1:s2].astype(BF16),
        "wdt": jnp.pad(wi[:, s2:s3], ((0, 0), (0, pad))).astype(BF16),
        "wg": wi[:, s3:].astype(BF16),
        "cs": jnp.asarray(np.concatenate([c_ch, s_ch], axis=1), BF16),
        "gpos_seq": jnp.asarray(np.concatenate([c_seq, -s_seq], axis=1), BF16),
        "gpos_grid": jnp.asarray(np.block([[bc, -bs], [-bs, -bc]]), BF16),
        "c_rows": jnp.asarray(c_row, BF16), "s_rows": jnp.asarray(s_row, BF16),
        "conv_w": conv_w[i], "conv_b": row(conv_b[i]),
        "dtb": jnp.pad(row(dt_bias[i]), ((0, 0), (0, pad))),
        "alog": jnp.pad(row(A_log[i]), ((0, 0), (0, pad))),
        "dskip": jnp.repeat(D_skip[i], HEAD_DIM).reshape(1, D_INNER),
        "ng": row(ssd_norm_g[i]), "wso": w_ssd_out[i].astype(BF16),
        "wfo": w_fourier[i].astype(BF16), "wo": w_out[i].astype(BF16),
        "g2": row(norm2_g[i]), "w1": w_ff1[i].astype(BF16), "w2": w_ff2[i].astype(BF16),
        "gfin": row(final_norm_g),
    }
    tiles_per_seq = x_sample.shape[1] // TOKEN_TILE
    y_prompt, new_f, new_b = _layer(x_prompt, mod3, lambda t: 0, None, None, False, True, w)
    y_sample, _, _ = _layer(x_sample, mod3, lambda t: 1 + t // tiles_per_seq,
                            _prep_state(state_ssm_fwd[:, i]), _prep_state(state_ssm_bwd[:, i]),
                            True, False, w)
    return (y_prompt, y_sample, new_f, new_b)
```

```python
import functools

import numpy as np
import jax
import jax.numpy as jnp
from jax import lax
from jax.experimental import pallas as pl
from jax.experimental.pallas import tpu as pltpu

D_MODEL = 1024
GRID_W = 64
F_GROUPS = 4
F_GROUP_W = D_MODEL // F_GROUPS
F_WIDTH = F_GROUPS * F_GROUP_W
D_INNER = 2 * D_MODEL
HEAD_DIM = 64
N_HEADS = D_INNER // HEAD_DIM
N_BC_GROUPS = 4
HEADS_PER_GROUP = N_HEADS // N_BC_GROUPS
D_STATE = 128
CONV_W = 5
CHUNK = 128
D_FF = 4 * D_MODEL
N_MOD = 6
D_BC = N_BC_GROUPS * D_STATE
D_XBC = D_INNER + 2 * D_BC
EPS = 1e-6

LANES = 128
SUBLANES = 8
TOKEN_TILE = 256
HEADS_PER_DOT = 4
VMEM_LIMIT = 60000 * 1024

F32 = jnp.float32
BF16 = jnp.bfloat16


def _const_spec(shape):
    return pl.BlockSpec(shape, lambda *_: (0,) * len(shape), pipeline_mode=pl.Buffered(1))


def _params(n_axes):
    return pltpu.CompilerParams(dimension_semantics=("arbitrary",) * n_axes,
                                vmem_limit_bytes=VMEM_LIMIT)


def _dot(a, b):
    return jnp.dot(a, b, preferred_element_type=F32)


def _rms(x):
    return x * lax.rsqrt(jnp.mean(x * x, axis=-1, keepdims=True) + EPS)


def _split3(v):
    hi = v.astype(BF16)
    r1 = v - hi.astype(F32)
    mid = r1.astype(BF16)
    lo = (r1 - mid.astype(F32)).astype(BF16)
    return hi, mid, lo


def _dft(n):
    k = np.arange(n)
    ang = 2.0 * np.pi * np.outer(k, k) / n
    return np.cos(ang) / np.sqrt(n), np.sin(ang) / np.sqrt(n)


def _mod_kernel(c_ref, w_ref, b_ref, o_ref):
    c = c_ref[...]
    s = (c * jax.nn.sigmoid(c)).astype(BF16)
    o_ref[...] = _dot(s, w_ref[...].astype(BF16)) + b_ref[...]


def _mod(cvec, w_mod, b_mod):
    n = w_mod.shape[1]
    tn = 1536
    return pl.pallas_call(
        _mod_kernel,
        grid=(n // tn,),
        in_specs=[pl.BlockSpec((SUBLANES, D_MODEL), lambda j: (0, 0)),
                  pl.BlockSpec((D_MODEL, tn), lambda j: (0, j)),
                  pl.BlockSpec((1, tn), lambda j: (0, j))],
        out_specs=pl.BlockSpec((SUBLANES, tn), lambda j: (0, j)),
        out_shape=jax.ShapeDtypeStruct((SUBLANES, n), F32),
        compiler_params=_params(1),
        name="mod",
    )(cvec, w_mod, b_mod)


def _in_proj_kernel(n_f, tiles_per_seq, *refs):
    refs = list(refs)
    x_ref = refs.pop(0)
    halo = tiles_per_seq > 1
    if halo:
        xp_ref, xn_ref = refs.pop(0), refs.pop(0)
    (mod_ref, g1_ref, wuf_ref, wz_ref, wxbc_ref, wdt_ref, wg_ref, cs_ref, gpos_ref, cw_ref, cb_ref,
     z_ref, xs_ref, bc_ref, dt_ref, gate_ref) = refs[:16]
    f_refs = refs[16:16 + n_f]
    ext_ref = refs[16 + n_f]
    m = mod_ref[0]
    shift1, scale1 = m[0:1], m[1:2]

    def modnorm(v):
        return ((_rms(v) * g1_ref[...]) * (1.0 + scale1) + shift1).astype(BF16)

    h = modnorm(x_ref[...])
    z_ref[...] = _dot(h, wz_ref[...]).astype(BF16)
    dt_ref[...] = _dot(h, wdt_ref[...])
    gate_ref[...] = _dot(h, wg_ref[...]).astype(BF16)

    lo, hi = SUBLANES, SUBLANES + TOKEN_TILE
    if halo:
        ti = pl.program_id(0) % tiles_per_seq
        x_cat = jnp.concatenate([x_ref[...], xp_ref[...], xn_ref[...]], axis=0)
        xbc = _dot(modnorm(x_cat), wxbc_ref[...])
        ext_ref[lo:hi] = xbc[:TOKEN_TILE]
        ext_ref[0:lo] = jnp.where(ti > 0, xbc[TOKEN_TILE:hi], 0.0)
        ext_ref[hi:hi + SUBLANES] = jnp.where(ti < tiles_per_seq - 1, xbc[hi:], 0.0)
    else:
        ext_ref[lo:hi] = _dot(h, wxbc_ref[...])
        ext_ref[0:lo] = jnp.zeros((SUBLANES, D_XBC), F32)
        ext_ref[hi:hi + SUBLANES] = jnp.zeros((SUBLANES, D_XBC), F32)
    first = lo - CONV_W // 2
    acc = cb_ref[...] + cw_ref[0:1] * ext_ref[pl.ds(first, TOKEN_TILE), :]
    for k in range(1, CONV_W):
        acc = acc + cw_ref[k:k + 1] * ext_ref[pl.ds(first + k, TOKEN_TILE), :]
    xc = (acc * jax.nn.sigmoid(acc)).astype(BF16)
    xs_ref[...] = xc[:, :D_INNER]
    bc_ref[...] = xc[:, D_INNER:]

    uf = _dot(h, wuf_ref[...]).astype(BF16)
    ucs = [_dot(uf[:, k * F_GROUP_W:(k + 1) * F_GROUP_W], cs_ref[...]) for k in range(F_GROUPS)]
    uc = jnp.concatenate([t[:, :F_GROUP_W] for t in ucs], axis=1)
    us = jnp.concatenate([t[:, F_GROUP_W:] for t in ucs], axis=1)
    stacked = jnp.concatenate([uc, us], axis=0).astype(BF16)
    f = _dot(gpos_ref[...], stacked)
    for k in range(n_f):
        f_refs[k][...] = f[k * TOKEN_TILE:(k + 1) * TOKEN_TILE].astype(BF16)


def _in_proj(x2d, tiles_per_seq, mod3, mod_row, g1, wuf, wz, wxbc, wdt, wg, cs, gpos, conv_w, conv_b):
    t = x2d.shape[0]
    t8 = t // SUBLANES
    per = TOKEN_TILE // SUBLANES
    n_f = gpos.shape[0] // TOKEN_TILE
    row = lambda i: (i, 0)
    in_specs = [pl.BlockSpec((TOKEN_TILE, D_MODEL), row)]
    args = [x2d]
    if tiles_per_seq > 1:
        in_specs += [pl.BlockSpec((SUBLANES, D_MODEL), lambda i: (jnp.maximum(i * per - 1, 0), 0)),
                     pl.BlockSpec((SUBLANES, D_MODEL), lambda i: (jnp.minimum((i + 1) * per, t8 - 1), 0))]
        args += [x2d, x2d]
    consts = [g1, wuf, wz, wxbc, wdt, wg, cs, gpos, conv_w, conv_b]
    in_specs += [pl.BlockSpec((1, N_MOD, D_MODEL), lambda i: (mod_row(i), 0, 0))]
    in_specs += [_const_spec(a.shape) for a in consts]
    outs = [(D_INNER, BF16), (D_INNER, BF16), (2 * D_BC, BF16), (LANES, F32), (D_INNER, BF16)]
    outs += [(F_WIDTH, BF16)] * n_f
    return pl.pallas_call(
        functools.partial(_in_proj_kernel, n_f, tiles_per_seq),
        grid=(t // TOKEN_TILE,),
        in_specs=in_specs,
        out_specs=[pl.BlockSpec((TOKEN_TILE, w), row) for w, _ in outs],
        out_shape=[jax.ShapeDtypeStruct((t, w), d) for w, d in outs],
        scratch_shapes=[pltpu.VMEM((TOKEN_TILE + 2 * SUBLANES, D_XBC), F32)],
        compiler_params=_params(1),
        name="in_proj",
    )(*args, mod3, *consts)


def _row_dft_kernel(c_ref, s_ref, wr_ref, wi_ref, o_ref):
    o_ref[0] = (_dot(c_ref[...], wr_ref[0]) + _dot(s_ref[...], wi_ref[0])).astype(BF16)


def _row_dft(c32, s32, wr, wi):
    b, r, n = wr.shape
    tn = 8192
    blk = pl.BlockSpec((1, r, tn), lambda i, j: (i, 0, j))
    return pl.pallas_call(
        _row_dft_kernel,
        grid=(b, n // tn),
        in_specs=[_const_spec(c32.shape), _const_spec(s32.shape), blk, blk],
        out_specs=blk,
        out_shape=jax.ShapeDtypeStruct((b, r, n), BF16),
        compiler_params=_params(2),
        name="row_dft",
    )(c32, s32, wr, wi)


def _ssd_kernel(rev, nc, has_h0, want_state, *refs):
    refs = list(refs)
    xs_ref, bc_ref, dt_ref, dtb_ref, alog_ref = refs[:5]
    refs = refs[5:]
    extra_ref = refs.pop(0)
    h0_ref = refs.pop(0) if has_h0 else None
    y_ref = refs.pop(0)
    hfin_ref = refs.pop(0) if want_state else None
    (ht_ref,) = refs

    c = pl.program_id(1)
    last = 0 if rev else CHUNK - 1
    lo = N_HEADS if rev else 0

    @pl.when(c == 0)
    def _():
        ht_ref[...] = h0_ref[0] if has_h0 else jnp.zeros_like(ht_ref)

    xs_b = xs_ref[...]
    bm = bc_ref[:, :D_BC].astype(F32)
    cm_b = bc_ref[:, D_BC:]

    v = dt_ref[...] + dtb_ref[...]
    dt = jnp.maximum(v, 0.0) + jnp.log1p(jnp.exp(-jnp.abs(v)))
    a = dt * (-jnp.exp(alog_ref[...]))
    ii = lax.broadcasted_iota(jnp.int32, (CHUNK, CHUNK), 0)
    jj = lax.broadcasted_iota(jnp.int32, (CHUNK, CHUNK), 1)
    mask = (jj >= ii) if rev else (jj <= ii)
    tri = mask.astype(BF16)
    acum = sum(_dot(tri, p) for p in _split3(a))
    acum_t = acum.T
    dt_t = dt.T
    w_t = dt_t * jnp.exp(acum_t[:, last:last + 1] - acum_t)
    lane_lo = jj < HEAD_DIM

    for g in range(N_BC_GROUPS):
        gs = slice(g * D_STATE, (g + 1) * D_STATE)
        b_g = bm[:, gs]
        c_g = cm_b[:, gs]
        cb = lax.dot_general(c_g, b_g.astype(BF16), (((1,), (1,)), ((), ())),
                             preferred_element_type=F32)
        bt_g = b_g.T
        hcols = slice(g * HEADS_PER_GROUP * HEAD_DIM, (g + 1) * HEADS_PER_GROUP * HEAD_DIM)
        y_off = _dot(c_g, ht_ref[:, hcols].astype(BF16))
        for q in range(HEADS_PER_GROUP // HEADS_PER_DOT):
            h0 = g * HEADS_PER_GROUP + q * HEADS_PER_DOT
            ms, bts, ebs = [], [], []
            for r in range(HEADS_PER_DOT):
                hl = lo + h0 + r
                colb = jnp.broadcast_to(acum[:, hl:hl + 1], (CHUNK, CHUNK))
                seg = colb - acum_t[hl:hl + 1, :]
                l_mat = jnp.exp(jnp.where(mask, seg, -jnp.inf))
                ms.append((cb * l_mat * dt_t[hl:hl + 1, :]).astype(BF16))
                bts.append((bt_g * w_t[hl:hl + 1, :]).astype(BF16))
                ebs.append(jnp.exp(colb))
            cols = slice(h0 * HEAD_DIM, (h0 + HEADS_PER_DOT) * HEAD_DIM)
            xq = xs_b[:, cols]
            wq = HEADS_PER_DOT * HEAD_DIM
            ri = lax.broadcasted_iota(jnp.int32, (HEADS_PER_DOT * CHUNK, wq), 0) // CHUNK
            ci = lax.broadcasted_iota(jnp.int32, (HEADS_PER_DOT * CHUNK, wq), 1) // HEAD_DIM
            xbd = jnp.where(ri == ci, jnp.concatenate([xq] * HEADS_PER_DOT, axis=0),
                            jnp.zeros((), BF16))
            y_diag = _dot(jnp.concatenate(ms, axis=1), xbd)
            s_new = _dot(jnp.concatenate(bts, axis=1), xbd)
            e_exp = jnp.concatenate([jnp.where(lane_lo, ebs[0], ebs[1]),
                                     jnp.where(lane_lo, ebs[2], ebs[3])], axis=1)
            y = y_diag + y_off[:, q * wq:(q + 1) * wq] * e_exp
            if rev:
                y = y + extra_ref[:, cols].astype(F32)
            else:
                y = y + xq.astype(F32) * extra_ref[:, cols]
            y_ref[:, cols] = y.astype(BF16)
            ht_ref[:, cols] = ht_ref[:, cols] * e_exp[last:last + 1, :] + s_new

    if want_state:
        @pl.when(c == nc - 1)
        def _():
            hfin_ref[0] = ht_ref[...].T


def _ssd(xs, bc, dtp, dtb, alog, extra, h0, nseq, nc, rev, want_state):
    t = xs.shape[0]
    blk = (lambda s, c: (s * nc + (nc - 1 - c), 0)) if rev else (lambda s, c: (s * nc + c, 0))
    in_specs = [pl.BlockSpec((CHUNK, D_INNER), blk), pl.BlockSpec((CHUNK, 2 * D_BC), blk),
                pl.BlockSpec((CHUNK, LANES), blk), _const_spec(dtb.shape), _const_spec(alog.shape),
                pl.BlockSpec((CHUNK, D_INNER), blk) if rev else _const_spec(extra.shape)]
    args = [xs, bc, dtp, dtb, alog, extra]
    if h0 is not None:
        in_specs.append(pl.BlockSpec((1, D_STATE, D_INNER), lambda s, c: (s, 0, 0)))
        args.append(h0)
    out_specs = [pl.BlockSpec((CHUNK, D_INNER), blk)]
    out_shape = [jax.ShapeDtypeStruct((t, D_INNER), BF16)]
    if want_state:
        out_specs.append(pl.BlockSpec((1, D_INNER, D_STATE), lambda s, c: (s, 0, 0)))
        out_shape.append(jax.ShapeDtypeStruct((nseq, D_INNER, D_STATE), F32))
    return pl.pallas_call(
        functools.partial(_ssd_kernel, rev, nc, h0 is not None, want_state),
        grid=(nseq, nc),
        in_specs=in_specs,
        out_specs=out_specs,
        out_shape=out_shape,
        scratch_shapes=[pltpu.VMEM((D_STATE, D_INNER), F32)],
        compiler_params=_params(2),
        name="ssd_bwd" if rev else "ssd_fwd",
    )(*args)


def _mix_out_kernel(y_ref, z_ref, f_ref, gate_ref, x_ref, mod_ref, ng_ref,
                    wso_ref, wfo_ref, wo_ref, o_ref):
    z = z_ref[...].astype(F32)
    y = y_ref[...].astype(F32) * (z * jax.nn.sigmoid(z))
    yn = (_rms(y) * ng_ref[...]).astype(BF16)
    y_s = _dot(yn, wso_ref[...])
    y_f = _dot(f_ref[...], wfo_ref[...])
    gates = jax.nn.sigmoid(gate_ref[...].astype(F32))
    merged = (gates[:, :D_MODEL] * y_f + gates[:, D_MODEL:] * y_s).astype(BF16)
    gate1 = mod_ref[0][2:3]
    o_ref[...] = x_ref[...] + gate1 * _dot(merged, wo_ref[...])


def _mix_out(y, z, f, gates, x2d, mod3, mod_row, ng, wso, wfo, wo):
    t = x2d.shape[0]
    row = lambda i: (i, 0)
    return pl.pallas_call(
        _mix_out_kernel,
        grid=(t // TOKEN_TILE,),
        in_specs=[pl.BlockSpec((TOKEN_TILE, D_INNER), row),
                  pl.BlockSpec((TOKEN_TILE, D_INNER), row), pl.BlockSpec((TOKEN_TILE, F_WIDTH), row),
                  pl.BlockSpec((TOKEN_TILE, D_INNER), row), pl.BlockSpec((TOKEN_TILE, D_MODEL), row),
                  pl.BlockSpec((1, N_MOD, D_MODEL), lambda i: (mod_row(i), 0, 0)),
                  _const_spec(ng.shape), _const_spec(wso.shape), _const_spec(wfo.shape),
                  _const_spec(wo.shape)],
        out_specs=pl.BlockSpec((TOKEN_TILE, D_MODEL), row),
        out_shape=jax.ShapeDtypeStruct((t, D_MODEL), F32),
        compiler_params=_params(1),
        name="mix_out",
    )(y, z, f, gates, x2d, mod3, ng, wso, wfo, wo)


def _mlp_kernel(x_ref, mod_ref, g2_ref, w1_ref, w2_ref, gfin_ref, o_ref):
    x = x_ref[...]
    m = mod_ref[0]
    shift2, scale2, gate2 = m[3:4], m[4:5], m[5:6]
    h = ((_rms(x) * g2_ref[...]) * (1.0 + scale2) + shift2).astype(BF16)
    a = jnp.maximum(_dot(h, w1_ref[...]), 0.0)
    hid = (a * a).astype(BF16)
    x2 = x + gate2 * _dot(hid, w2_ref[...])
    o_ref[...] = _rms(x2) * gfin_ref[...]


def _mlp(x2d, mod3, mod_row, g2, w1, w2, gfin):
    t = x2d.shape[0]
    row = lambda i: (i, 0)
    return pl.pallas_call(
        _mlp_kernel,
        grid=(t // TOKEN_TILE,),
        in_specs=[pl.BlockSpec((TOKEN_TILE, D_MODEL), row),
                  pl.BlockSpec((1, N_MOD, D_MODEL), lambda i: (mod_row(i), 0, 0)),
                  _const_spec(g2.shape), _const_spec(w1.shape), _const_spec(w2.shape),
                  _const_spec(gfin.shape)],
        out_specs=pl.BlockSpec((TOKEN_TILE, D_MODEL), row),
        out_shape=jax.ShapeDtypeStruct((t, D_MODEL), F32),
        compiler_params=_params(1),
        name="mlp",
    )(x2d, mod3, g2, w1, w2, gfin)


def _layer(x3d, mod3, mod_row, h0_f, h0_b, grid, want_state, w):
    nseq, seq, _ = x3d.shape
    x2d = x3d.reshape(nseq * seq, D_MODEL)
    nc = seq // CHUNK
    gpos = w["gpos_grid"] if grid else w["gpos_seq"]
    z, xs, bc, dtp, gates, *fs = _in_proj(x2d, seq // TOKEN_TILE, mod3, mod_row, w["g1"], w["wuf"],
                                          w["wz"], w["wxbc"], w["wdt"], w["wg"], w["cs"], gpos,
                                          w["conv_w"], w["conv_b"])
    if grid:
        rows = seq // GRID_W
        wr = fs[0].reshape(nseq, rows, GRID_W * F_WIDTH)
        wi = fs[1].reshape(nseq, rows, GRID_W * F_WIDTH)
        f = _row_dft(w["c_rows"], w["s_rows"], wr, wi).reshape(nseq * seq, F_WIDTH)
    else:
        f = fs[0]
    out_f = _ssd(xs, bc, dtp, w["dtb"], w["alog"], w["dskip"], h0_f, nseq, nc, False, want_state)
    out_b = _ssd(xs, bc, dtp, w["dtb"], w["alog"], out_f[0], h0_b, nseq, nc, True, want_state)
    x1 = _mix_out(out_b[0], z, f, gates, x2d, mod3, mod_row, w["ng"], w["wso"], w["wfo"], w["wo"])
    y = _mlp(x1, mod3, mod_row, w["g2"], w["w1"], w["w2"], w["gfin"]).reshape(x3d.shape)
    if want_state:
        shape = (nseq, 1, N_HEADS, HEAD_DIM, D_STATE)
        return y, out_f[1].reshape(shape), out_b[1].reshape(shape)
    return y, None, None


def _prep_state(s):
    b = s.shape[0]
    return jnp.transpose(s, (0, 3, 1, 2)).reshape(b, D_STATE, D_INNER)


def kernel(x_prompt, x_sample, state_ssm_fwd, state_ssm_bwd, c, c_ctx, w_mod, b_mod, norm1_g, w_in,
           w_fourier, conv_w, conv_b, dt_bias, A_log, D_skip, ssd_norm_g, w_ssd_out, w_out, norm2_g,
           w_ff1, w_ff2, final_norm_g):
    depth = w_mod.shape[0]
    assert depth == 1
    i = 0
    nb = c.shape[0]
    cvec = jnp.concatenate([c_ctx[None], c, jnp.zeros((SUBLANES - 1 - nb, D_MODEL), F32)], axis=0)
    mod = _mod(cvec, w_mod[i], b_mod[i][None])
    mod3 = mod[:1 + nb].reshape(1 + nb, N_MOD, D_MODEL)

    c_ch, s_ch = _dft(F_GROUP_W)
    c_seq, s_seq = _dft(x_prompt.shape[1])
    c_col, s_col = _dft(GRID_W)
    rows = x_sample.shape[1] // GRID_W
    c_row, s_row = _dft(rows)
    eye = np.eye(TOKEN_TILE // GRID_W)
    bc, bs = np.kron(eye, c_col), np.kron(eye, s_col)
    wi = w_in[i]
    s0, s1, s2, s3 = F_WIDTH, F_WIDTH + D_INNER, F_WIDTH + D_INNER + D_XBC, F_WIDTH + D_INNER + D_XBC + 2 * N_HEADS
    pad = LANES - 2 * N_HEADS
    row = lambda v: v.reshape(1, -1).astype(F32)
    w = {
        "g1": row(norm1_g[i]),
        "wuf": wi[:, :s0].astype(BF16), "wz": wi[:, s0:s1].astype(BF16),
        "wxbc": wi[:, s1:s2].astype(BF16),
        "wdt": jnp.pad(wi[:, s2:s3], ((0, 0), (0, pad))).astype(BF16),
        "wg": wi[:, s3:].astype(BF16),
        "cs": jnp.asarray(np.concatenate([c_ch, s_ch], axis=1), BF16),
        "gpos_seq": jnp.asarray(np.concatenate([c_seq, -s_seq], axis=1), BF16),
        "gpos_grid": jnp.asarray(np.block([[bc, -bs], [-bs, -bc]]), BF16),
        "c_rows": jnp.asarray(c_row, BF16), "s_rows": jnp.asarray(s_row, BF16),
        "conv_w": conv_w[i], "conv_b": row(conv_b[i]),
        "dtb": jnp.pad(row(dt_bias[i]), ((0, 0), (0, pad))),
        "alog": jnp.pad(row(A_log[i]), ((0, 0), (0, pad))),
        "dskip": jnp.repeat(D_skip[i], HEAD_DIM).reshape(1, D_INNER),
        "ng": row(ssd_norm_g[i]), "wso": w_ssd_out[i].astype(BF16),
        "wfo": w_fourier[i].astype(BF16), "wo": w_out[i].astype(BF16),
        "g2": row(norm2_g[i]), "w1": w_ff1[i].astype(BF16), "w2": w_ff2[i].astype(BF16),
        "gfin": row(final_norm_g),
    }
    tiles_per_seq = x_sample.shape[1] // TOKEN_TILE
    y_prompt, new_f, new_b = _layer(x_prompt, mod3, lambda t: 0, None, None, False, True, w)
    y_sample, _, _ = _layer(x_sample, mod3, lambda t: 1 + t // tiles_per_seq,
                            _prep_state(state_ssm_fwd[:, i]), _prep_state(state_ssm_bwd[:, i]),
                            True, False, w)
    return (y_prompt, y_sample, new_f, new_b)
```
